```python
import jax, jax.numpy as jnp
from jax import lax
import numpy as np

D_MODEL = 1024
BATCH = 32
SEQ = 256
DEPTH = 1
DEC_BATCH = 8
DEC_SEQ = 4096
PAST_LEN = 256

GRID_W = 64
MIX_W = D_MODEL
ATT_W = MIX_W // 2
RET_W = MIX_W - ATT_W
HEAD_DIM = 64
N_Q_HEADS = ATT_W // HEAD_DIM
N_KV_HEADS = 2
Q_PER_KV = N_Q_HEADS // N_KV_HEADS
KV_W = N_KV_HEADS * HEAD_DIM
N_RET_HEADS = 4
RET_DV = RET_W // N_RET_HEADS
RET_DK = RET_DV // 2
RET_QK_W = N_RET_HEADS * RET_DK
ROPE_THETA = 10000.0
Q_BLOCK = 128
RET_CHUNK = 128
N_EXPERTS = 16
EXPERT_FF = 1024
CAPACITY_FACTOR = 2
EPS = 1e-6
IN_COLS = ATT_W + 2 * KV_W + 2 * RET_QK_W + 2 * RET_W
SPLIT_POINTS = (ATT_W, ATT_W + KV_W, ATT_W + 2 * KV_W, ATT_W + 2 * KV_W + RET_QK_W,
                ATT_W + 2 * KV_W + 2 * RET_QK_W, ATT_W + 2 * KV_W + 2 * RET_QK_W + RET_W)

kernel_name = 'hymba_dit_retention_ec_step'


def rmsnorm(x, w):
    xf = x.astype(jnp.float32)
    y = xf * lax.rsqrt(jnp.mean(xf * xf, axis=-1, keepdims=True) + EPS)
    return (y * w.astype(jnp.float32)).astype(x.dtype)


def rope_2d(x):
    T = x.shape[1]
    rows = T // GRID_W
    half = HEAD_DIM // 2
    nf = half // 2
    freqs = 1.0 / (ROPE_THETA ** (jnp.arange(nf, dtype=jnp.float32) / nf))
    r_pos = jnp.repeat(jnp.arange(rows, dtype=jnp.float32), GRID_W)
    c_pos = jnp.tile(jnp.arange(GRID_W, dtype=jnp.float32), rows)
    xf = x.astype(jnp.float32)

    def rot(xa, pos):
        ang = pos[:, None] * freqs[None, :]
        cos = jnp.cos(ang)[None, :, None, :]
        sin = jnp.sin(ang)[None, :, None, :]
        x1, x2 = xa[..., :nf], xa[..., nf:]
        return jnp.concatenate([x1 * cos - x2 * sin, x2 * cos + x1 * sin], axis=-1)

    out = jnp.concatenate([rot(xf[..., :half], r_pos), rot(xf[..., half:], c_pos)], axis=-1)
    return out.astype(x.dtype)


def block_attention(q, k, v):
    B, T = q.shape[:2]
    nb = T // Q_BLOCK
    qb = q.reshape(B, nb, Q_BLOCK, N_KV_HEADS, Q_PER_KV, HEAD_DIM).transpose(1, 0, 3, 4, 2, 5)
    scale = HEAD_DIM ** -0.5

    def one_block(qblk):
        s = jnp.einsum('bkgqd,bskd->bkgqs', qblk, k).astype(jnp.float32) * scale
        p = jax.nn.softmax(s, axis=-1).astype(v.dtype)
        return jnp.einsum('bkgqs,bskd->bkgqd', p, v)

    o = lax.map(one_block, qb)
    return o.transpose(1, 0, 4, 2, 3, 5).reshape(B, T, N_Q_HEADS * HEAD_DIM)


def retention_scan(q, k, v, log_gamma, s0, strict):
    B, H, T, _ = q.shape
    nc = T // RET_CHUNK
    n = jnp.arange(RET_CHUNK, dtype=jnp.float32)
    lg = log_gamma.astype(jnp.float32)[:, None]
    diff = n[:, None] - n[None, :]
    mask = (diff > 0) if strict else (diff >= 0)
    dmat = jnp.where(mask[None], jnp.exp(lg[:, :, None] * jnp.maximum(diff, 0.0)[None]), 0.0)
    q_decay = jnp.exp(lg * (n + 1.0))[None, :, :, None]
    k_decay = jnp.exp(lg * (RET_CHUNK - 1.0 - n))[None, :, :, None]
    chunk_decay = jnp.exp(lg * RET_CHUNK)[None, :, :, None]

    def to_chunks(a):
        return a.astype(jnp.float32).reshape(B, H, nc, RET_CHUNK, a.shape[-1]).transpose(2, 0, 1, 3, 4)

    def step(s, inp):
        qc, kc, vc = inp
        scores = jnp.einsum('bhqd,bhkd->bhqk', qc, kc) * dmat[None]
        o = (jnp.einsum('bhqk,bhkv->bhqv', scores, vc)
             + jnp.einsum('bhqd,bhdv->bhqv', qc, s) * q_decay)
        s = s * chunk_decay + jnp.einsum('bhkd,bhkv->bhdv', kc * k_decay, vc)
        return s, o

    s_final, o = lax.scan(step, s0.astype(jnp.float32), (to_chunks(q), to_chunks(k), to_chunks(v)))
    o = o.transpose(1, 2, 0, 3, 4).reshape(B, H, T, v.shape[-1])
    return o, s_final


def bi_retention(q, k, v, log_gammas, s0_f, s0_b):
    o_f, s_f = retention_scan(q, k, v, log_gammas[0], s0_f, strict=False)
    flip = lambda a: jnp.flip(a, axis=2)
    o_b, s_b = retention_scan(flip(q), flip(k), flip(v), log_gammas[1], s0_b, strict=True)
    return o_f + flip(o_b), s_f, s_b


def mixer(h, lw, latent, ctx_k, ctx_v, s0_f, s0_b):
    B, T, _ = h.shape
    proj = jnp.einsum('btd,dc->btc', h, lw['w_in'])
    qa, ka, va, qr, kr, vr, gr = jnp.split(proj, SPLIT_POINTS, axis=-1)
    qa = rmsnorm(qa.reshape(B, T, N_Q_HEADS, HEAD_DIM), lw['q_norm_w'])
    ka = rmsnorm(ka.reshape(B, T, N_KV_HEADS, HEAD_DIM), lw['k_norm_w'])
    va = va.reshape(B, T, N_KV_HEADS, HEAD_DIM)
    if latent:
        keys = jnp.concatenate([rope_2d(ka), ctx_k.astype(ka.dtype)], axis=1)
        vals = jnp.concatenate([va, ctx_v.astype(va.dtype)], axis=1)
        att = block_attention(rope_2d(qa), keys, vals)
    else:
        att = block_attention(qa, ka, va)
    qr = qr.reshape(B, T, N_RET_HEADS, RET_DK).transpose(0, 2, 1, 3)
    kr = (kr * (RET_DK ** -0.5)).reshape(B, T, N_RET_HEADS, RET_DK).transpose(0, 2, 1, 3)
    vr = vr.reshape(B, T, N_RET_HEADS, RET_DV).transpose(0, 2, 1, 3)
    log_gammas = -jnp.exp(lw['ret_decay_logit'].astype(jnp.float32))
    ret, s_f, s_b = bi_retention(qr, kr, vr, log_gammas, s0_f, s0_b)
    ret = rmsnorm(ret.transpose(0, 2, 1, 3), lw['ret_gn_w'])
    ret = jax.nn.silu(gr) * ret.reshape(B, T, RET_W).astype(h.dtype)
    out = jnp.einsum('btc,cd->btd', jnp.concatenate([att, ret], axis=-1), lw['w_out'])
    return out, ka, va, s_f, s_b


def expert_choice_ffn(h, lw):
    B, T, D = h.shape
    cap = CAPACITY_FACTOR * T // N_EXPERTS
    aff = jax.nn.softmax(jnp.einsum('btd,de->bte', h, lw['w_router']).astype(jnp.float32), axis=-1)
    gates, idx = lax.top_k(aff.transpose(0, 2, 1), cap)
    xg = jax.vmap(lambda xb, ib: xb[ib])(h, idx)
    a = jnp.einsum('becd,edf->becf', xg, lw['w_gate'])
    u = jnp.einsum('becd,edf->becf', xg, lw['w_up'])
    y = jnp.einsum('becf,efd->becd', jax.nn.silu(a) * u, lw['w_down'])
    y = y * gates[..., None].astype(y.dtype)
    return jax.vmap(lambda yb, ib: jnp.zeros((T, D), yb.dtype).at[ib.reshape(-1)].add(yb.reshape(-1, D)))(y, idx)


def trunk_layer(x, cvec, lw, latent, ctx_k, ctx_v, s0_f, s0_b):
    mod = jnp.einsum('bd,de->be', jax.nn.silu(cvec), lw['w_mod']) + lw['b_mod']
    shift1, scale1, gate1, shift2, scale2, gate2 = jnp.split(mod[:, None, :].astype(x.dtype), 6, axis=-1)
    h = rmsnorm(x, lw['norm1_w']) * (1.0 + scale1) + shift1
    mix, k_ctx, v_ctx, s_f, s_b = mixer(h, lw, latent, ctx_k, ctx_v, s0_f, s0_b)
    x = x + gate1 * mix
    h = rmsnorm(x, lw['norm2_w']) * (1.0 + scale2) + shift2
    x = x + gate2 * expert_choice_ffn(h, lw)
    return x, k_ctx, v_ctx, s_f, s_b


def setup_inputs(seed: int = 0) -> dict:
    key = jax.random.key(seed)
    ks = jax.random.split(key, 24)
    f32 = jnp.float32

    def nrm(k, shape, scale):
        return jax.random.normal(k, shape, f32) * scale

    base_decay = np.log(-np.log(1.0 - 2.0 ** (-5.0 - np.arange(N_RET_HEADS)))).astype(np.float32)
    return {
        'x_prompt': nrm(ks[0], (BATCH, SEQ, D_MODEL), 1.0),
        'x_sample': nrm(ks[1], (DEC_BATCH, DEC_SEQ, D_MODEL), 1.0),
        'cache_attn_k': nrm(ks[2], (DEC_BATCH, DEPTH, PAST_LEN, N_KV_HEADS, HEAD_DIM), 1.0),
        'cache_attn_v': nrm(ks[3], (DEC_BATCH, DEPTH, PAST_LEN, N_KV_HEADS, HEAD_DIM), 1.0),
        'state_ret': nrm(ks[4], (DEC_BATCH, DEPTH, 2, N_RET_HEADS, RET_DK, RET_DV), 0.5),
        'c': nrm(ks[5], (DEC_BATCH, D_MODEL), 1.0),
        'c_ctx': nrm(ks[6], (D_MODEL,), 1.0),
        'w_mod': nrm(ks[7], (DEPTH, D_MODEL, 6 * D_MODEL), D_MODEL ** -0.5),
        'b_mod': nrm(ks[8], (DEPTH, 6 * D_MODEL), 0.02),
        'norm1_w': 1.0 + nrm(ks[9], (DEPTH, D_MODEL), 0.02),
        'norm2_w': 1.0 + nrm(ks[10], (DEPTH, D_MODEL), 0.02),
        'w_in': nrm(ks[11], (DEPTH, D_MODEL, IN_COLS), D_MODEL ** -0.5),
        'q_norm_w': 1.0 + nrm(ks[12], (DEPTH, HEAD_DIM), 0.02),
        'k_norm_w': 1.0 + nrm(ks[13], (DEPTH, HEAD_DIM), 0.02),
        'ret_decay_logit': jnp.asarray(base_decay)[None, None, :] + nrm(ks[14], (DEPTH, 2, N_RET_HEADS), 0.1),
        'ret_gn_w': 1.0 + nrm(ks[15], (DEPTH, N_RET_HEADS, RET_DV), 0.02),
        'w_out': nrm(ks[16], (DEPTH, MIX_W, D_MODEL), MIX_W ** -0.5),
        'w_router': nrm(ks[17], (DEPTH, D_MODEL, N_EXPERTS), D_MODEL ** -0.5),
        'w_gate': nrm(ks[18], (DEPTH, N_EXPERTS, D_MODEL, EXPERT_FF), D_MODEL ** -0.5),
        'w_up': nrm(ks[19], (DEPTH, N_EXPERTS, D_MODEL, EXPERT_FF), D_MODEL ** -0.5),
        'w_down': nrm(ks[20], (DEPTH, N_EXPERTS, EXPERT_FF, D_MODEL), EXPERT_FF ** -0.5),
        'norm_f_w': 1.0 + nrm(ks[21], (D_MODEL,), 0.02),
    }


def reference(x_prompt, x_sample, cache_attn_k, cache_attn_v, state_ret, c, c_ctx,
              w_mod, b_mod, norm1_w, norm2_w, w_in, q_norm_w, k_norm_w, ret_decay_logit,
              ret_gn_w, w_out, w_router, w_gate, w_up, w_down, norm_f_w):
    def layer_weights(l):
        return {'w_mod': w_mod[l], 'b_mod': b_mod[l], 'norm1_w': norm1_w[l], 'norm2_w': norm2_w[l],
                'w_in': w_in[l], 'q_norm_w': q_norm_w[l], 'k_norm_w': k_norm_w[l],
                'ret_decay_logit': ret_decay_logit[l], 'ret_gn_w': ret_gn_w[l], 'w_out': w_out[l],
                'w_router': w_router[l], 'w_gate': w_gate[l], 'w_up': w_up[l], 'w_down': w_down[l]}

    ctx = x_prompt
    B = ctx.shape[0]
    zero_state = jnp.zeros((B, N_RET_HEADS, RET_DK, RET_DV), jnp.float32)
    ks_out, vs_out, st_out = [], [], []
    for l in range(DEPTH):
        ctx, k_l, v_l, sf_l, sb_l = trunk_layer(ctx, c_ctx[None, :], layer_weights(l), False,
                                                None, None, zero_state, zero_state)
        ks_out.append(k_l)
        vs_out.append(v_l)
        st_out.append(jnp.stack([sf_l, sb_l], axis=1))
    y_prompt = rmsnorm(ctx, norm_f_w)
    new_attn_k = jnp.stack(ks_out, axis=1)
    new_attn_v = jnp.stack(vs_out, axis=1)
    new_ret_state = jnp.stack(st_out, axis=1)

    x = x_sample
    for l in range(DEPTH):
        x, _, _, _, _ = trunk_layer(x, c, layer_weights(l), True,
                                    cache_attn_k[:, l], cache_attn_v[:, l],
                                    state_ret[:, l, 0], state_ret[:, l, 1])
    y_sample = rmsnorm(x, norm_f_w)

    return (y_prompt, y_sample, new_attn_k, new_attn_v, new_ret_state)
```

```python
import functools

import jax
import jax.numpy as jnp
from jax import lax
from jax.experimental import pallas as pl
from jax.experimental.pallas import tpu as pltpu

F32 = jnp.float32
BF16 = jnp.bfloat16

D_MODEL = 1024
GRID_W = 64
HEAD_DIM = 64
N_Q_HEADS = 8
N_KV_HEADS = 2
Q_PER_KV = N_Q_HEADS // N_KV_HEADS
ATT_W = N_Q_HEADS * HEAD_DIM
KV_W = N_KV_HEADS * HEAD_DIM
N_RET_HEADS = 4
RET_DV = 128
RET_DK = 64
RET_QK_W = N_RET_HEADS * RET_DK
RET_W = N_RET_HEADS * RET_DV
ROPE_THETA = 10000.0
RET_CHUNK = 128
N_EXPERTS = 16
EXPERT_FF = 1024
CAPACITY_FACTOR = 2
EPS = 1e-6

LANES = 128
VMEM_LIMIT_BYTES = 56 * 1024 * 1024

TOKEN_TILE = 512
ATT_Q_BLOCK = 256
ATT_KEY_CHUNK = 256
SEG_TOKENS = 4096
SEG_ROWS = CAPACITY_FACTOR * SEG_TOKENS // N_EXPERTS
SCATTER_UNROLL = 4
MOD_ROWS = 16

_C_QA = 0
_C_KA = ATT_W
_C_VA = _C_KA + KV_W
_C_QR = _C_VA + KV_W
_C_KR = _C_QR + RET_QK_W
_C_VR = _C_KR + RET_QK_W
_C_GR = _C_VR + RET_W
_C_END = _C_GR + RET_W

_T_QA = 0
_T_VA = _T_QA + ATT_W
_T_QR = _T_VA + KV_W
_T_VR = _T_QR + RET_QK_W
_T_GR = _T_VR + RET_W
_T_END = _T_GR + RET_W


def _cparams(semantics, **kw):
    return pltpu.CompilerParams(dimension_semantics=semantics,
                                vmem_limit_bytes=VMEM_LIMIT_BYTES, **kw)


def _split_bf16(x):
    hi = x.astype(BF16)
    lo = (x - hi.astype(F32)).astype(BF16)
    return hi, lo


def _dot(a, b):
    return jnp.dot(a, b, preferred_element_type=F32)


def _dot_nt(a, b):
    return lax.dot_general(a, b, (((1,), (1,)), ((), ())), preferred_element_type=F32)


def _dot_tn(a, b):
    return lax.dot_general(a, b, (((0,), (0,)), ((), ())), preferred_element_type=F32)


def _silu(x):
    return x * (1.0 / (1.0 + jnp.exp(-x)))


def _mod_kernel(c_ref, w_ref, b_ref, o_ref):
    s_hi, s_lo = _split_bf16(_silu(c_ref[...]))
    w_hi, w_lo = _split_bf16(w_ref[...])
    o_ref[...] = _dot(s_hi, w_hi) + _dot(s_lo, w_hi) + _dot(s_hi, w_lo) + b_ref[...]


def _modulation(cvec, w_mod, b_mod):
    n = w_mod.shape[1]
    blk = D_MODEL
    return pl.pallas_call(
        _mod_kernel,
        grid=(n // blk,),
        in_specs=[pl.BlockSpec((MOD_ROWS, D_MODEL), lambda j: (0, 0)),
                  pl.BlockSpec((D_MODEL, blk), lambda j: (0, j)),
                  pl.BlockSpec((1, blk), lambda j: (0, j))],
        out_specs=pl.BlockSpec((MOD_ROWS, blk), lambda j: (0, j)),
        out_shape=jax.ShapeDtypeStruct((MOD_ROWS, n), F32),
        compiler_params=_cparams(("arbitrary",)),
        name="modulation",
    )(cvec, w_mod, b_mod.reshape(1, n))


def _mod_row(latent, tiles_per_batch):
    i = pl.program_id(0)
    return 1 + i // tiles_per_batch if latent else 0


def _rms_rows(x):
    return x * lax.rsqrt(jnp.mean(x * x, axis=-1, keepdims=True) + EPS)


def _inproj_kernel(latent, tiles_per_batch, *refs):
    if latent:
        (x_ref, mod_ref, n1w_ref, wnat_ref, wt_ref, qnw_ref, knw_ref,
         cost_ref, sint_ref, cosn_ref, sinn_ref,
         qT_ref, k_ref, vT_ref, qrT_ref, kr_ref, vrT_ref, grT_ref) = refs
    else:
        (x_ref, mod_ref, n1w_ref, wnat_ref, wt_ref, qnw_ref, knw_ref,
         qT_ref, k_ref, vT_ref, qrT_ref, kr_ref, vrT_ref, grT_ref,
         kf_ref, vf_ref) = refs
    row = _mod_row(latent, tiles_per_batch)
    shift1 = mod_ref[pl.ds(row, 1), 0:D_MODEL]
    scale1 = mod_ref[pl.ds(row, 1), D_MODEL:2 * D_MODEL]
    h = _rms_rows(x_ref[...]) * n1w_ref[...] * (1.0 + scale1) + shift1
    hb = h.astype(BF16)
    nat = _dot(hb, wnat_ref[...])
    pT = _dot_nt(wt_ref[...], hb)

    for hd in range(N_Q_HEADS):
        xh = pT[_T_QA + HEAD_DIM * hd:_T_QA + HEAD_DIM * (hd + 1), :]
        ms = jnp.mean(xh * xh, axis=0, keepdims=True)
        xh = xh * lax.rsqrt(ms + EPS) * qnw_ref[...]
        if latent:
            sw = jnp.concatenate([xh[16:32], xh[0:16], xh[48:64], xh[32:48]], axis=0)
            xh = xh * cost_ref[...] + sw * sint_ref[...]
        qT_ref[HEAD_DIM * hd:HEAD_DIM * (hd + 1), :] = xh.astype(BF16)
    vT_ref[...] = pT[_T_VA:_T_QR, :].astype(BF16)
    qrT_ref[...] = pT[_T_QR:_T_VR, :].astype(BF16)
    vrT_ref[...] = pT[_T_VR:_T_GR, :].astype(BF16)
    grT_ref[...] = pT[_T_GR:_T_END, :].astype(BF16)

    ka = nat[:, 0:KV_W]
    r = lax.broadcasted_iota(jnp.int32, (KV_W, KV_W), 0) // HEAD_DIM
    c = lax.broadcasted_iota(jnp.int32, (KV_W, KV_W), 1) // HEAD_DIM
    seg = jnp.where(r == c, 1.0 / HEAD_DIM, 0.0).astype(BF16)
    sq_hi, sq_lo = _split_bf16(ka * ka)
    ms = _dot(sq_hi, seg) + _dot(sq_lo, seg)
    kn = ka * lax.rsqrt(ms + EPS) * knw_ref[...]
    if latent:
        lane = lax.broadcasted_iota(jnp.int32, kn.shape, 1)
        sw = jnp.where((lane // 16) % 2 == 0,
                       pltpu.roll(kn, KV_W - 16, axis=1), pltpu.roll(kn, 16, axis=1))
        kn = kn * cosn_ref[...] + sw * sinn_ref[...]
    else:
        kf_ref[...] = kn
        vf_ref[...] = nat[:, KV_W + RET_QK_W:KV_W + RET_QK_W + KV_W]
    k_ref[...] = kn.astype(BF16)
    kr_ref[...] = (nat[:, KV_W:KV_W + RET_QK_W] * (RET_DK ** -0.5)).astype(BF16)


def _inproj(x, mod, n1w, wnat, wt, qnw, knw, rope, latent, tokens_per_batch):
    ntok = x.shape[0]
    tm = TOKEN_TILE
    nt = ntok // tm
    tpb = max(tokens_per_batch // tm, 1)
    full = lambda a: pl.BlockSpec(a.shape, lambda i: (0,) * a.ndim)
    in_specs = [pl.BlockSpec((tm, D_MODEL), lambda i: (i, 0)), full(mod), full(n1w),
                full(wnat), full(wt), full(qnw), full(knw)]
    args = [x, mod, n1w, wnat, wt, qnw, knw]
    if latent:
        cost, sint, cosn, sinn = rope
        in_specs += [pl.BlockSpec((HEAD_DIM, tm), lambda i: (0, i % tpb)),
                     pl.BlockSpec((HEAD_DIM, tm), lambda i: (0, i % tpb)),
                     pl.BlockSpec((tm, KV_W), lambda i: (i % tpb, 0)),
                     pl.BlockSpec((tm, KV_W), lambda i: (i % tpb, 0))]
        args += [cost, sint, cosn, sinn]
    tspec = lambda rows: pl.BlockSpec((rows, tm), lambda i: (0, i))
    nspec = lambda cols: pl.BlockSpec((tm, cols), lambda i: (i, 0))
    out_specs = [tspec(ATT_W), nspec(KV_W), tspec(KV_W), tspec(RET_QK_W),
                 nspec(RET_QK_W), tspec(RET_W), tspec(RET_W)]
    out_shape = [jax.ShapeDtypeStruct((ATT_W, ntok), BF16),
                 jax.ShapeDtypeStruct((ntok, KV_W), BF16),
                 jax.ShapeDtypeStruct((KV_W, ntok), BF16),
                 jax.ShapeDtypeStruct((RET_QK_W, ntok), BF16),
                 jax.ShapeDtypeStruct((ntok, RET_QK_W), BF16),
                 jax.ShapeDtypeStruct((RET_W, ntok), BF16),
                 jax.ShapeDtypeStruct((RET_W, ntok), BF16)]
    if not latent:
        out_specs += [nspec(KV_W), nspec(KV_W)]
        out_shape += [jax.ShapeDtypeStruct((ntok, KV_W), F32)] * 2
    return pl.pallas_call(
        functools.partial(_inproj_kernel, latent, tpb),
        grid=(nt,), in_specs=in_specs, out_specs=out_specs, out_shape=out_shape,
        compiler_params=_cparams(("parallel",)),
        name="inproj_lat" if latent else "inproj_ctx",
    )(*args)


def _attn_kernel(n_self, has_cache, *refs):
    if has_cache:
        qT_ref, k_ref, vT_ref, kc_ref, vcT_ref, o_ref = refs
    else:
        qT_ref, k_ref, vT_ref, o_ref = refs
    tq = qT_ref.shape[1]
    sc = ATT_KEY_CHUNK
    ncol = Q_PER_KV * tq
    for kv in range(N_KV_HEADS):
        heads = [Q_PER_KV * kv + g for g in range(Q_PER_KV)]
        top = jnp.concatenate(
            [qT_ref[HEAD_DIM * hd:HEAD_DIM * (hd + 1), :] for hd in heads], axis=1)
        zero = jnp.zeros_like(top)
        qz = jnp.concatenate([top, zero] if kv == 0 else [zero, top], axis=0)
        vrows = slice(HEAD_DIM * kv, HEAD_DIM * (kv + 1))

        def step(kc, vtc, carry):
            m, l, acc = carry
            s = _dot(kc, qz)
            m_new = jnp.maximum(m, jnp.max(s, axis=0, keepdims=True))
            alpha = jnp.exp(m - m_new)
            p = jnp.exp(s - m_new)
            l = alpha * l + jnp.sum(p, axis=0, keepdims=True)
            acc = alpha * acc + _dot(vtc, p.astype(BF16))
            return m_new, l, acc

        def body(j, carry):
            off = pl.multiple_of(j * sc, sc)
            return step(k_ref[pl.ds(off, sc), :], vT_ref[vrows, pl.ds(off, sc)], carry)

        carry = (jnp.full((1, ncol), -1e30, F32), jnp.zeros((1, ncol), F32),
                 jnp.zeros((HEAD_DIM, ncol), F32))
        carry = lax.fori_loop(0, n_self, body, carry)
        if has_cache:
            carry = step(kc_ref[0], vcT_ref[0, vrows, :], carry)
        _, l, acc = carry
        o = acc / l
        for g, hd in enumerate(heads):
            o_ref[HEAD_DIM * hd:HEAD_DIM * (hd + 1), :] = o[:, g * tq:(g + 1) * tq].astype(BF16)


def _attention(qT, k, vT, seq, cache=None):
    ntok = qT.shape[1]
    nb = ntok // seq
    tq = min(ATT_Q_BLOCK, seq)
    nq = seq // tq
    in_specs = [pl.BlockSpec((ATT_W, tq), lambda b, i: (0, b * nq + i)),
                pl.BlockSpec((seq, KV_W), lambda b, i: (b, 0)),
                pl.BlockSpec((KV_W, seq), lambda b, i: (0, b))]
    args = [qT, k, vT]
    if cache is not None:
        kc, vcT = cache
        past = kc.shape[1]
        in_specs += [pl.BlockSpec((1, past, KV_W), lambda b, i: (b, 0, 0)),
                     pl.BlockSpec((1, KV_W, past), lambda b, i: (b, 0, 0))]
        args += [kc, vcT]
    return pl.pallas_call(
        functools.partial(_attn_kernel, seq // ATT_KEY_CHUNK, cache is not None),
        grid=(nb, nq), in_specs=in_specs,
        out_specs=pl.BlockSpec((ATT_W, tq), lambda b, i: (0, b * nq + i)),
        out_shape=jax.ShapeDtypeStruct((ATT_W, ntok), BF16),
        compiler_params=_cparams(("parallel", "parallel")),
        name="attention_lat" if cache is not None else "attention_ctx",
    )(*args)


def _ret_kernel(nc, has_s0, emit_state, *refs):
    refs = list(refs)
    k_ref, qT_ref, vT_ref, gT_ref, lg_ref, gnw_ref = refs[:6]
    pos = 6
    s0_ref = None
    if has_s0:
        s0_ref = refs[pos]
        pos += 1
    o_ref = refs[pos]
    pos += 1
    st_ref = None
    if emit_state:
        st_ref = refs[pos]
        pos += 1
    stb_ref = refs[pos]
    C = RET_CHUNK
    ri = lax.broadcasted_iota(jnp.int32, (C, C), 0).astype(F32)
    ci = lax.broadcasted_iota(jnp.int32, (C, C), 1).astype(F32)
    rows_pair = lax.broadcasted_iota(jnp.int32, (2 * RET_DK, C), 0) // RET_DK
    for hh in range(2):
        lgf = -jnp.exp(lg_ref[0, hh, 0:1, 0:1])
        lgb = -jnp.exp(lg_ref[1, hh, 0:1, 0:1])
        dT = jnp.where(ci >= ri, jnp.exp(lgf * jnp.maximum(ci - ri, 0.0)),
                       jnp.exp(lgb * jnp.maximum(ri - ci, 0.0)))
        qdf = jnp.exp(lgf * (ci[0:1, :] + 1.0))
        qdb = jnp.exp(lgb * (C - ci[0:1, :]))
        kdf = jnp.exp(lgf * (C - 1.0 - ri))
        kdb = jnp.exp(lgb * ri)
        cdf = jnp.exp(lgf * C)
        cdb = jnp.exp(lgb * C)
        vrows = slice(RET_DV * hh, RET_DV * (hh + 1))

        def kv_outer(c0, kd):
            kc = (k_ref[pl.ds(c0, C), :].astype(F32) * kd).astype(BF16)
            return _dot(vT_ref[vrows, pl.ds(c0, C)], kc)

        if has_s0:
            sf0 = s0_ref[0, 0, hh]
            sb0 = s0_ref[0, 1, hh]
        else:
            sf0 = jnp.zeros((RET_DV, 2 * RET_DK), F32)
            sb0 = sf0

        def bstep(t, sb):
            c = nc - 1 - t
            c0 = pl.multiple_of(c * C, C)
            stb_ref[hh, c] = sb
            return sb * cdb + kv_outer(c0, kdb)

        sb_fin = lax.fori_loop(0, nc, bstep, sb0)

        def fstep(c, sf):
            c0 = pl.multiple_of(c * C, C)
            qz = jnp.where(rows_pair == hh, qT_ref[:, pl.ds(c0, C)], jnp.zeros((), BF16))
            sT = _dot(k_ref[pl.ds(c0, C), :], qz)
            vt = vT_ref[vrows, pl.ds(c0, C)]
            oT = _dot(vt, (sT * dT).astype(BF16))
            oT = oT + _dot(sf.astype(BF16), qz) * qdf
            oT = oT + _dot(stb_ref[hh, c].astype(BF16), qz) * qdb
            ms = jnp.mean(oT * oT, axis=0, keepdims=True)
            y = oT * lax.rsqrt(ms + EPS) * gnw_ref[hh]
            g = gT_ref[vrows, pl.ds(c0, C)].astype(F32)
            o_ref[vrows, pl.ds(c0, C)] = (_silu(g) * y).astype(BF16)
            return sf * cdf + kv_outer(c0, kdf)

        sf_fin = lax.fori_loop(0, nc, fstep, sf0)
        if emit_state:
            krows = slice(RET_DK * hh, RET_DK * (hh + 1))
            st_ref[0, 0, 0, hh] = sf_fin.T[krows, :]
            st_ref[0, 0, 1, hh] = sb_fin.T[krows, :]


def _retention(kr, qrT, vrT, grT, lg_b, gnw_b, seq, s0T=None, emit_state=False):
    ntok = kr.shape[0]
    nb = ntok // seq
    nc = seq // RET_CHUNK
    pair = 2 * RET_DK
    in_specs = [pl.BlockSpec((seq, pair), lambda b, p: (b, p)),
                pl.BlockSpec((pair, seq), lambda b, p: (p, b)),
                pl.BlockSpec((2 * RET_DV, seq), lambda b, p: (p, b)),
                pl.BlockSpec((2 * RET_DV, seq), lambda b, p: (p, b)),
                pl.BlockSpec((2, 2, 8, LANES), lambda b, p: (0, p, 0, 0)),
                pl.BlockSpec((2, RET_DV, LANES), lambda b, p: (p, 0, 0))]
    args = [kr, qrT, vrT, grT, lg_b, gnw_b]
    if s0T is not None:
        in_specs.append(pl.BlockSpec((1, 2, 2, RET_DV, pair), lambda b, p: (b, 0, p, 0, 0)))
        args.append(s0T)
    out_specs = [pl.BlockSpec((2 * RET_DV, seq), lambda b, p: (p, b))]
    out_shape = [jax.ShapeDtypeStruct((RET_W, ntok), BF16)]
    if emit_state:
        out_specs.append(pl.BlockSpec((1, 1, 2, 2, RET_DK, RET_DV),
                                      lambda b, p: (b, 0, 0, p, 0, 0)))
        out_shape.append(jax.ShapeDtypeStruct((nb, 1, 2, N_RET_HEADS, RET_DK, RET_DV), F32))
    return pl.pallas_call(
        functools.partial(_ret_kernel, nc, s0T is not None, emit_state),
        grid=(nb, 2), in_specs=in_specs, out_specs=out_specs, out_shape=out_shape,
        scratch_shapes=[pltpu.VMEM((2, nc, RET_DV, pair), F32)],
        compiler_params=_cparams(("parallel", "parallel")),
        name="retention_lat" if s0T is not None else "retention_ctx",
    )(*args)


def _outproj_kernel(latent, tiles_per_batch, x_ref, attT_ref, retT_ref, mod_ref, n2w_ref,
                    woa_ref, wor_ref, wrh_ref, wrl_ref, x1_ref, h2_ref, affT_ref):
    row = _mod_row(latent, tiles_per_batch)
    gate1 = mod_ref[pl.ds(row, 1), 2 * D_MODEL:3 * D_MODEL]
    shift2 = mod_ref[pl.ds(row, 1), 3 * D_MODEL:4 * D_MODEL]
    scale2 = mod_ref[pl.ds(row, 1), 4 * D_MODEL:5 * D_MODEL]
    mix = _dot_tn(attT_ref[...], woa_ref[...]) + _dot_tn(retT_ref[...], wor_ref[...])
    x1 = x_ref[...] + gate1 * mix
    x1_ref[...] = x1
    h2 = _rms_rows(x1) * n2w_ref[...] * (1.0 + scale2) + shift2
    h2_ref[...] = h2
    h_hi, h_lo = _split_bf16(h2)
    logits = _dot(h_hi, wrh_ref[...]) + _dot(h_lo, wrh_ref[...]) + _dot(h_hi, wrl_ref[...])
    lt = logits.T[0:N_EXPERTS, :]
    e = jnp.exp(lt - jnp.max(lt, axis=0, keepdims=True))
    affT_ref[...] = e / jnp.sum(e, axis=0, keepdims=True)


def _outproj(x, attT, retT, mod, n2w, woa, wor, wrh, wrl, latent, tokens_per_batch):
    ntok = x.shape[0]
    tm = TOKEN_TILE
    tpb = max(tokens_per_batch // tm, 1)
    full = lambda a: pl.BlockSpec(a.shape, lambda i: (0,) * a.ndim)
    return pl.pallas_call(
        functools.partial(_outproj_kernel, latent, tpb),
        grid=(ntok // tm,),
        in_specs=[pl.BlockSpec((tm, D_MODEL), lambda i: (i, 0)),
                  pl.BlockSpec((ATT_W, tm), lambda i: (0, i)),
                  pl.BlockSpec((RET_W, tm), lambda i: (0, i)),
                  full(mod), full(n2w), full(woa), full(wor), full(wrh), full(wrl)],
        out_specs=[pl.BlockSpec((tm, D_MODEL), lambda i: (i, 0)),
                   pl.BlockSpec((tm, D_MODEL), lambda i: (i, 0)),
                   pl.BlockSpec((N_EXPERTS, tm), lambda i: (0, i))],
        out_shape=[jax.ShapeDtypeStruct((ntok, D_MODEL), F32),
                   jax.ShapeDtypeStruct((ntok, D_MODEL), F32),
                   jax.ShapeDtypeStruct((N_EXPERTS, ntok), F32)],
        compiler_params=_cparams(("parallel",)),
        name="outproj_lat" if latent else "outproj_ctx",
    )(x, attT, retT, mod, n2w, woa, wor, wrh, wrl)


def _count_ge(x, thr):
    return jnp.sum(jnp.where(x >= thr, 1.0, 0.0), axis=1, keepdims=True)


def _topk_kernel(cap, aff_ref, idx_ref, gate_ref, slot_ref):
    x = aff_ref[...]
    ne, T = x.shape
    capf = float(cap)
    nchunk = T // LANES

    def bis(_, lh):
        lo, hi = lh
        mid = 0.5 * (lo + hi)
        ge = _count_ge(x, mid) >= capf
        return jnp.where(ge, mid, lo), jnp.where(ge, hi, mid)

    lo, hi = lax.fori_loop(0, 48, bis, (jnp.zeros((ne, 1), F32), jnp.full((ne, 1), 2.0, F32)))

    def below(hi):
        v = jnp.max(jnp.where(x < hi, x, -1.0), axis=1, keepdims=True)
        return v, _count_ge(x, v)

    def peel_cond(s):
        _, _, cnt = s
        return jnp.min(cnt) < capf

    def peel(s):
        hi, v, cnt = s
        hi = jnp.where(cnt < capf, v, hi)
        v, cnt = below(hi)
        return hi, v, cnt

    v0, c0 = below(hi)
    _, vstar, _ = lax.while_loop(peel_cond, peel, (hi, v0, c0))

    gt = x > vstar
    eq = x == vstar
    need = capf - jnp.sum(jnp.where(gt, 1.0, 0.0), axis=1, keepdims=True)

    tri = (lax.broadcasted_iota(jnp.int32, (LANES, LANES), 0)
           <= lax.broadcasted_iota(jnp.int32, (LANES, LANES), 1)).astype(BF16)

    def prefix_exclusive(mask):
        m = mask.astype(BF16)
        out = []
        run = jnp.zeros((ne, 1), F32)
        for j in range(nchunk):
            inc = _dot(m[:, j * LANES:(j + 1) * LANES], tri)
            out.append(inc - mask[:, j * LANES:(j + 1) * LANES] + run)
            run = run + inc[:, LANES - 1:LANES]
        return jnp.concatenate(out, axis=1)

    eqf = jnp.where(eq, 1.0, 0.0)
    sel = jnp.logical_or(gt, jnp.logical_and(eq, prefix_exclusive(eqf) < need))
    self_ = jnp.where(sel, 1.0, 0.0)
    slot = jnp.where(sel, prefix_exclusive(self_), -1.0)

    slot_ref[...] = slot
    tok = lax.broadcasted_iota(jnp.int32, (1, T), 1).astype(F32)
    width = min(cap, LANES)
    ncb = cap // width

    def invert(i, _):
        e = i // ncb
        off = pl.multiple_of((i % ncb) * width, width)
        srow = slot_ref[pl.ds(e, 1), :]
        xrow = aff_ref[pl.ds(e, 1), :]
        cids = lax.broadcasted_iota(jnp.int32, (LANES, 1), 0).astype(F32) + off.astype(F32)
        hit = srow == cids
        ti = jnp.sum(jnp.where(hit, tok, 0.0), axis=1, keepdims=True)
        gi = jnp.sum(jnp.where(hit, xrow, 0.0), axis=1, keepdims=True)
        ti = jnp.broadcast_to(ti, (LANES, LANES)).T[0:1, 0:width]
        gi = jnp.broadcast_to(gi, (LANES, LANES)).T[0:1, 0:width]
        idx_ref[0, i] = ti.astype(jnp.int32)
        gate_ref[0, i] = gi
        return 0

    lax.fori_loop(0, ne * ncb, invert, 0)


def _topk(affT, seq):
    ntok = affT.shape[1]
    nb = ntok // seq
    cap = CAPACITY_FACTOR * seq // N_EXPERTS
    width = min(cap, LANES)
    nrow = N_EXPERTS * cap // width
    ospec = lambda: pl.BlockSpec((1, nrow, 1, width), lambda b: (b, 0, 0, 0))
    idx, gate = pl.pallas_call(
        functools.partial(_topk_kernel, cap),
        grid=(nb,),
        in_specs=[pl.BlockSpec((N_EXPERTS, seq), lambda b: (0, b))],
        out_specs=[ospec(), ospec()],
        out_shape=[jax.ShapeDtypeStruct((nb, nrow, 1, width), jnp.int32),
                   jax.ShapeDtypeStruct((nb, nrow, 1, width), F32)],
        scratch_shapes=[pltpu.VMEM((N_EXPERTS, seq), F32)],
        compiler_params=_cparams(("parallel",)),
        name="topk_s%d" % seq,
    )(affT)
    return idx.reshape(nb, N_EXPERTS, cap), gate.reshape(nb, N_EXPERTS, cap)


def _expert_kernel(n_ctx_seg, rows_ref, hctx_ref, hlat_ref, wg_ref, wu_ref, wd_ref,
                   y_ref, wgb_ref, wub_ref, wdb_ref, xg_ref, sem):
    s = pl.program_id(1)

    @pl.when(s == 0)
    def _():
        wgb_ref[...] = wg_ref[0].astype(BF16)
        wub_ref[...] = wu_ref[0].astype(BF16)
        wdb_ref[...] = wd_ref[0].astype(BF16)

    def row_copy(src_ref, base, c):
        r = base + rows_ref[0, 0, 0, c]
        return pltpu.make_async_copy(src_ref.at[pl.ds(r, 1), :], xg_ref.at[pl.ds(c, 1), :], sem)

    def gather(src_ref, base):
        def start(c, _):
            row_copy(src_ref, base, c).start()
            return 0
        lax.fori_loop(0, SEG_ROWS, start, 0)

        def wait(c, _):
            row_copy(src_ref, base, c).wait()
            return 0
        lax.fori_loop(0, SEG_ROWS, wait, 0)

    @pl.when(s < n_ctx_seg)
    def _():
        gather(hctx_ref, s * SEG_TOKENS)

    @pl.when(s >= n_ctx_seg)
    def _():
        gather(hlat_ref, (s - n_ctx_seg) * SEG_TOKENS)

    xb = xg_ref[...].astype(BF16)
    a = _dot(xb, wgb_ref[...])
    u = _dot(xb, wub_ref[...])
    y_ref[0] = _dot((_silu(a) * u).astype(BF16), wdb_ref[...])


def _experts(rows, h2_ctx, h2_lat, w_gate, w_up, w_down):
    ne, nseg, _, _ = rows.shape
    n_ctx_seg = h2_ctx.shape[0] // SEG_TOKENS
    wspec = lambda: pl.BlockSpec((1, D_MODEL, EXPERT_FF), lambda e, s: (e, 0, 0))
    return pl.pallas_call(
        functools.partial(_expert_kernel, n_ctx_seg),
        grid=(ne, nseg),
        in_specs=[pl.BlockSpec((1, 1, 1, SEG_ROWS), lambda e, s: (e, s, 0, 0),
                               memory_space=pltpu.SMEM),
                  pl.BlockSpec(memory_space=pl.ANY),
                  pl.BlockSpec(memory_space=pl.ANY),
                  wspec(), wspec(),
                  pl.BlockSpec((1, EXPERT_FF, D_MODEL), lambda e, s: (e, 0, 0))],
        out_specs=pl.BlockSpec((1, SEG_ROWS, D_MODEL), lambda e, s: (e, s, 0)),
        out_shape=jax.ShapeDtypeStruct((ne, nseg * SEG_ROWS, D_MODEL), F32),
        scratch_shapes=[pltpu.VMEM((D_MODEL, EXPERT_FF), BF16),
                        pltpu.VMEM((D_MODEL, EXPERT_FF), BF16),
                        pltpu.VMEM((EXPERT_FF, D_MODEL), BF16),
                        pltpu.VMEM((SEG_ROWS, D_MODEL), F32),
                        pltpu.SemaphoreType.DMA(())],
        compiler_params=_cparams(("arbitrary", "arbitrary")),
        name="experts",
    )(rows, h2_ctx, h2_lat, w_gate, w_up, w_down)


def _combine_kernel(rows_ref, gates_ref, y_ref, acc_ref):
    e = pl.program_id(1)

    @pl.when(e == 0)
    def _():
        acc_ref[...] = jnp.zeros_like(acc_ref)

    def group(gi, _):
        c0 = gi * SCATTER_UNROLL
        toks = [rows_ref[0, 0, 0, c0 + u] for u in range(SCATTER_UNROLL)]
        sums = [acc_ref[pl.ds(toks[u], 1), :]
                + gates_ref[0, 0, 0, c0 + u] * y_ref[0, pl.ds(c0 + u, 1), :]
                for u in range(SCATTER_UNROLL)]
        for u in range(SCATTER_UNROLL):
            acc_ref[pl.ds(toks[u], 1), :] = sums[u]
        return 0

    lax.fori_loop(0, SEG_ROWS // SCATTER_UNROLL, group, 0)


def _combine(rows_se, gates_se, y):
    nseg, ne, _, _ = rows_se.shape
    sspec = lambda: pl.BlockSpec((1, 1, 1, SEG_ROWS), lambda s, e: (s, e, 0, 0),
                                 memory_space=pltpu.SMEM)
    return pl.pallas_call(
        _combine_kernel,
        grid=(nseg, ne),
        in_specs=[sspec(), sspec(),
                  pl.BlockSpec((1, SEG_ROWS, D_MODEL), lambda s, e: (e, s, 0))],
        out_specs=pl.BlockSpec((SEG_TOKENS, D_MODEL), lambda s, e: (s, 0)),
        out_shape=jax.ShapeDtypeStruct((nseg * SEG_TOKENS, D_MODEL), F32),
        compiler_params=_cparams(("parallel", "arbitrary")),
        name="combine",
    )(rows_se, gates_se, y)


def _final_kernel(latent, tiles_per_batch, x1_ref, f_ref, mod_ref, nfw_ref, o_ref):
    row = _mod_row(latent, tiles_per_batch)
    gate2 = mod_ref[pl.ds(row, 1), 5 * D_MODEL:6 * D_MODEL]
    o_ref[...] = _rms_rows(x1_ref[...] + gate2 * f_ref[...]) * nfw_ref[...]


def _final(x1, ffn, seg_offset, mod, nfw, latent, tokens_per_batch):
    ntok = x1.shape[0]
    tm = TOKEN_TILE
    tpb = max(tokens_per_batch // tm, 1)
    off = seg_offset // tm
    full = lambda a: pl.BlockSpec(a.shape, lambda i: (0,) * a.ndim)
    return pl.pallas_call(
        functools.partial(_final_kernel, latent, tpb),
        grid=(ntok // tm,),
        in_specs=[pl.BlockSpec((tm, D_MODEL), lambda i: (i, 0)),
                  pl.BlockSpec((tm, D_MODEL), lambda i: (i + off, 0)),
                  full(mod), full(nfw)],
        out_specs=pl.BlockSpec((tm, D_MODEL), lambda i: (i, 0)),
        out_shape=jax.ShapeDtypeStruct((ntok, D_MODEL), F32),
        compiler_params=_cparams(("parallel",)),
        name="final_lat" if latent else "final_ctx",
    )(x1, ffn, mod, nfw)


def _rope_tables(seq):
    half = HEAD_DIM // 2
    nf = half // 2
    freqs = 1.0 / (ROPE_THETA ** (jnp.arange(nf, dtype=F32) / nf))
    t = jnp.arange(seq)
    r_pos = (t // GRID_W).astype(F32)
    c_pos = (t % GRID_W).astype(F32)
    ang_r = r_pos[:, None] * freqs[None, :]
    ang_c = c_pos[:, None] * freqs[None, :]
    cos = jnp.concatenate([jnp.cos(ang_r), jnp.cos(ang_r), jnp.cos(ang_c), jnp.cos(ang_c)], axis=1)
    sin = jnp.concatenate([-jnp.sin(ang_r), jnp.sin(ang_r), -jnp.sin(ang_c), jnp.sin(ang_c)], axis=1)
    cosn = jnp.concatenate([cos] * N_KV_HEADS, axis=1)
    sinn = jnp.concatenate([sin] * N_KV_HEADS, axis=1)
    return cos.T, sin.T, cosn, sinn


def kernel(x_prompt, x_sample, cache_attn_k, cache_attn_v, state_ret, c, c_ctx, w_mod, b_mod,
           norm1_w, norm2_w, w_in, q_norm_w, k_norm_w, ret_decay_logit, ret_gn_w, w_out,
           w_router, w_gate, w_up, w_down, norm_f_w):
    assert w_mod.shape[0] == 1, "single-layer kernel"
    nb_ctx, seq_ctx, _ = x_prompt.shape
    nb_lat, seq_lat, _ = x_sample.shape
    ntok_ctx = nb_ctx * seq_ctx
    ntok_lat = nb_lat * seq_lat
    assert ntok_ctx % SEG_TOKENS == 0 and seq_lat == SEG_TOKENS and SEG_TOKENS % seq_ctx == 0
    assert 1 + nb_lat <= MOD_ROWS

    cvec = jnp.zeros((MOD_ROWS, D_MODEL), F32).at[0].set(c_ctx).at[1:1 + nb_lat].set(c)
    w_in0 = w_in[0]
    wt = jnp.concatenate([w_in0[:, _C_QA:_C_KA], w_in0[:, _C_VA:_C_QR], w_in0[:, _C_QR:_C_KR],
                          w_in0[:, _C_VR:_C_END]], axis=1).T.astype(BF16)
    wnat_lat = jnp.concatenate([w_in0[:, _C_KA:_C_VA], w_in0[:, _C_KR:_C_VR]], axis=1).astype(BF16)
    wnat_ctx = jnp.concatenate([w_in0[:, _C_KA:_C_VA], w_in0[:, _C_KR:_C_VR],
                                w_in0[:, _C_VA:_C_QR]], axis=1).astype(BF16)
    qnw = jnp.broadcast_to((q_norm_w[0] * (HEAD_DIM ** -0.5))[:, None], (HEAD_DIM, TOKEN_TILE))
    knw = jnp.tile(k_norm_w[0], N_KV_HEADS)[None, :]
    n1w = norm1_w[0][None, :]
    n2w = norm2_w[0][None, :]
    nfw = norm_f_w[None, :]
    woa = w_out[0, :ATT_W].astype(BF16)
    wor = w_out[0, ATT_W:].astype(BF16)
    wr = jnp.zeros((D_MODEL, LANES), F32).at[:, :N_EXPERTS].set(w_router[0])
    wrh, wrl = _split_bf16(wr)
    lg_b = jnp.broadcast_to(ret_decay_logit[0][:, :, None, None], (2, N_RET_HEADS, 8, LANES))
    gnw_b = jnp.broadcast_to(ret_gn_w[0][:, :, None], (N_RET_HEADS, RET_DV, LANES))
    rope = _rope_tables(seq_lat)

    mod = _modulation(cvec, w_mod[0], b_mod[0])

    xc = x_prompt.reshape(ntok_ctx, D_MODEL)
    xl = x_sample.reshape(ntok_lat, D_MODEL)

    qT, k, vT, qrT, kr, vrT, grT, kf, vf = _inproj(
        xc, mod, n1w, wnat_ctx, wt, qnw, knw, None, False, seq_ctx)
    attT_c = _attention(qT, k, vT, seq_ctx)
    retT_c, new_state = _retention(kr, qrT, vrT, grT, lg_b, gnw_b, seq_ctx, emit_state=True)
    x1_c, h2_c, affT_c = _outproj(xc, attT_c, retT_c, mod, n2w, woa, wor, wrh, wrl, False, seq_ctx)

    qT, k, vT, qrT, kr, vrT, grT = _inproj(
        xl, mod, n1w, wnat_lat, wt, qnw, knw, rope, True, seq_lat)
    past = cache_attn_k.shape[2]
    kc = cache_attn_k[:, 0].reshape(nb_lat, past, KV_W).astype(BF16)
    vcT = cache_attn_v[:, 0].reshape(nb_lat, past, KV_W).transpose(0, 2, 1).astype(BF16)
    attT_l = _attention(qT, k, vT, seq_lat, cache=(kc, vcT))
    s0 = jnp.swapaxes(state_ret[:, 0], -1, -2)
    zeros = jnp.zeros_like(s0)
    even = (jnp.arange(N_RET_HEADS) % 2 == 0)[None, None, :, None, None]
    s0T = jnp.concatenate([jnp.where(even, s0, zeros), jnp.where(even, zeros, s0)], axis=-1)
    retT_l = _retention(kr, qrT, vrT, grT, lg_b, gnw_b, seq_lat, s0T=s0T)[0]
    x1_l, h2_l, affT_l = _outproj(xl, attT_l, retT_l, mod, n2w, woa, wor, wrh, wrl, True, seq_lat)

    idx_c, gate_c = _topk(affT_c, seq_ctx)
    idx_l, gate_l = _topk(affT_l, seq_lat)
    per_seg = SEG_TOKENS // seq_ctx
    n_ctx_seg = ntok_ctx // SEG_TOKENS
    idx_c = idx_c + (jnp.arange(nb_ctx) % per_seg * seq_ctx)[:, None, None]

    def ctx_rows(a):
        a = a.reshape(n_ctx_seg, per_seg, N_EXPERTS, -1).transpose(2, 0, 1, 3)
        return a.reshape(N_EXPERTS, n_ctx_seg, SEG_ROWS)

    rows = jnp.concatenate([ctx_rows(idx_c), idx_l.transpose(1, 0, 2)], axis=1)[:, :, None, :]
    gates = jnp.concatenate([ctx_rows(gate_c), gate_l.transpose(1, 0, 2)], axis=1)[:, :, None, :]

    y = _experts(rows, h2_c, h2_l, w_gate[0], w_up[0], w_down[0])
    ffn = _combine(rows.transpose(1, 0, 2, 3), gates.transpose(1, 0, 2, 3), y)

    y_prompt = _final(x1_c, ffn, 0, mod, nfw, False, seq_ctx).reshape(x_prompt.shape)
    y_sample = _final(x1_l, ffn, ntok_ctx, mod, nfw, True, seq_lat).reshape(x_sample.shape)
    new_k = kf.reshape(nb_ctx, 1, seq_ctx, N_KV_HEADS, HEAD_DIM)
    new_v = vf.reshape(nb_ctx, 1, seq_ctx, N_KV_HEADS, HEAD_DIM)
    return (y_prompt, y_sample, new_k, new_v, new_state)
```
